```python
import jax, jax.numpy as jnp
from jax import lax
import numpy as np

D_MODEL = 1024
BATCH = 8
SEQ = 2048
DEPTH = 4
DEC_BATCH = 32
DEC_SEQ = 8
PAST_LEN = 16384
PAGE_SIZE = 128

BRANCH_WIDTH = D_MODEL // 2
N_BRANCH = 3
LRU_WIDTH = BRANCH_WIDTH
LRU_BLOCKS = 8
LRU_BLOCK = LRU_WIDTH // LRU_BLOCKS
CONV_WIDTH = 4
LRU_C = 8.0
MLA_HEADS = 8
QK_NOPE = 64
QK_ROPE = 32
QK_HEAD = QK_NOPE + QK_ROPE
V_HEAD = BRANCH_WIDTH // MLA_HEADS
Q_LORA = D_MODEL // 4
KV_LORA = D_MODEL // 4
ROPE_THETA = 10000.0
Q_BLOCK = 128
ATTN_SCALE = QK_HEAD ** -0.5
CHUNK = 128
SGU_WIDTH = BRANCH_WIDTH
SGU_GROUPS = 4
SGU_GROUP_DIM = SGU_WIDTH // SGU_GROUPS
D_FF = ((8 * D_MODEL + 3 * 256 - 1) // (3 * 256)) * 256
EPS = 1e-6
NEG_INF = -1e30
OFF_CQ = LRU_WIDTH
OFF_CKV = OFF_CQ + Q_LORA
OFF_KPE = OFF_CKV + KV_LORA
OFF_U = OFF_KPE + QK_ROPE
OFF_V = OFF_U + SGU_WIDTH
OFF_G = OFF_V + SGU_WIDTH
IN_WIDTH = OFF_G + N_BRANCH * D_MODEL

kernel_name = 'hybrid_rglru_mla_chunkmlp_decode_step'


def rms_norm(x, g):
    xf = x.astype(jnp.float32)
    y = xf * lax.rsqrt(jnp.mean(xf * xf, axis=-1, keepdims=True) + EPS)
    return (y * g.astype(jnp.float32)).astype(x.dtype)


def layer_norm(x, g, b):
    xf = x.astype(jnp.float32)
    mu = jnp.mean(xf, axis=-1, keepdims=True)
    xc = xf - mu
    var = jnp.mean(xc * xc, axis=-1, keepdims=True)
    return (xc * lax.rsqrt(var + EPS) * g.astype(jnp.float32) + b.astype(jnp.float32)).astype(x.dtype)


def rope(x, pos):
    half = QK_ROPE // 2
    inv = ROPE_THETA ** (-jnp.arange(half, dtype=jnp.float32) / half)
    ang = pos.astype(jnp.float32)[:, None] * inv[None, :]
    ang = ang.reshape((ang.shape[0],) + (1,) * (x.ndim - 3) + (half,))
    cos, sin = jnp.cos(ang), jnp.sin(ang)
    xf = x.astype(jnp.float32)
    x1, x2 = xf[..., :half], xf[..., half:]
    return jnp.concatenate([x1 * cos - x2 * sin, x1 * sin + x2 * cos], axis=-1).astype(x.dtype)


def rglru_branch(xr, conv_prev, h0, pos, conv_w, conv_b, wa, ba, wx, bx, lam):
    bsz, t, _ = xr.shape
    xpad = jnp.concatenate([conv_prev.astype(xr.dtype), xr], axis=1)
    xc = conv_b + sum(xpad[:, k:k + t] * conv_w[k] for k in range(CONV_WIDTH))
    conv_new = xpad[:, t:]
    xb = xc.reshape(bsz, t, LRU_BLOCKS, LRU_BLOCK)
    r = jax.nn.sigmoid(jnp.einsum('btnc,ncd->btnd', xb, wa) + ba).reshape(bsz, t, LRU_WIDTH)
    i = jax.nn.sigmoid(jnp.einsum('btnc,ncd->btnd', xb, wx) + bx).reshape(bsz, t, LRU_WIDTH)
    log_a = -LRU_C * r.astype(jnp.float32) * jax.nn.softplus(-lam.astype(jnp.float32))
    a = jnp.exp(log_a)
    mult = jnp.sqrt(-jnp.expm1(2.0 * log_a))
    mult = jnp.where((pos == 0)[None, :, None], 1.0, mult)
    b = mult * (i * xc).astype(jnp.float32)

    def step(h, ab):
        h = ab[0] * h + ab[1]
        return h, h

    h_last, hs = lax.scan(step, h0.astype(jnp.float32), (jnp.swapaxes(a, 0, 1), jnp.swapaxes(b, 0, 1)))
    return jnp.swapaxes(hs, 0, 1).astype(xr.dtype), h_last.astype(xr.dtype), conv_new


def expand_kv(latent, w_ukv, k_nope_norm):
    kv = (latent @ w_ukv).reshape(latent.shape[:-1] + (MLA_HEADS, QK_NOPE + V_HEAD))
    return rms_norm(kv[..., :QK_NOPE], k_nope_norm), kv[..., QK_NOPE:]


def attend(q_nope, q_pe, k_nope, k_pe, v, mask):
    s = (jnp.einsum('...qhd,...khd->...hqk', q_nope, k_nope)
         + jnp.einsum('...qhr,...kr->...hqk', q_pe, k_pe)).astype(jnp.float32) * ATTN_SCALE
    p = jax.nn.softmax(jnp.where(mask, s, NEG_INF), axis=-1).astype(v.dtype)
    return jnp.einsum('...hqk,...khd->...qhd', p, v)


def mla_prompt(q_nope, q_pe, latent, k_pe, w_ukv, k_nope_norm):
    bsz, s, _, _ = q_nope.shape
    k_nope, v = expand_kv(latent, w_ukv, k_nope_norm)
    nb = s // Q_BLOCK
    qn = jnp.swapaxes(q_nope.reshape(bsz, nb, Q_BLOCK, MLA_HEADS, QK_NOPE), 0, 1)
    qp = jnp.swapaxes(q_pe.reshape(bsz, nb, Q_BLOCK, MLA_HEADS, QK_ROPE), 0, 1)
    k_pos = jnp.arange(s)

    def block(args):
        bi, qn_b, qp_b = args
        q_pos = bi * Q_BLOCK + jnp.arange(Q_BLOCK)
        mask = k_pos[None, :] <= q_pos[:, None]
        return attend(qn_b, qp_b, k_nope, k_pe, v, mask)

    o = lax.map(block, (jnp.arange(nb), qn, qp))
    return jnp.swapaxes(o, 0, 1).reshape(bsz, s, MLA_HEADS * V_HEAD)


def mla_sample(q_nope, q_pe, latent_new, k_pe_new, cache_lat, cache_kpe, page_table, layer, w_ukv, k_nope_norm):
    bsz, t, _, _ = q_nope.shape
    past = page_table.shape[1] * PAGE_SIZE
    mask = jnp.arange(past + t)[None, :] <= past + jnp.arange(t)[:, None]

    def one(args):
        pages, qn, qp, lat_new, kp_new = args
        lat = jnp.concatenate([cache_lat[layer, pages].reshape(past, KV_LORA).astype(lat_new.dtype), lat_new], axis=0)
        kp = jnp.concatenate([cache_kpe[layer, pages].reshape(past, QK_ROPE).astype(kp_new.dtype), kp_new], axis=0)
        k_nope, v = expand_kv(lat, w_ukv, k_nope_norm)
        return attend(qn, qp, k_nope, kp, v, mask)

    o = lax.map(one, (page_table, q_nope, q_pe, latent_new, k_pe_new))
    return o.reshape(bsz, t, MLA_HEADS * V_HEAD)


def sgu_branch(u, v_raw, ln_g, ln_b, w_s, b_s):
    v = layer_norm(v_raw, ln_g, ln_b)
    bsz, t, _ = v.shape
    nc = -(-t // CHUNK)
    vp = jnp.pad(v, ((0, 0), (0, nc * CHUNK - t), (0, 0))).reshape(bsz, nc, CHUNK, SGU_GROUPS, SGU_GROUP_DIM)
    tri = jnp.arange(CHUNK)[:, None] >= jnp.arange(CHUNK)[None, :]
    w = jnp.where(tri, w_s, 0)
    s = jnp.einsum('gts,bcsgd->bctgd', w, vp) + jnp.swapaxes(b_s, 0, 1)[:, :, None]
    s = s.reshape(bsz, nc * CHUNK, SGU_WIDTH)[:, :t]
    return u * s, v


def trunk_layer(x, pos, conv_prev, h0, lp, paged=None):
    bsz, t, _ = x.shape
    h = rms_norm(x, lp['attn_norm'])
    z = h @ lp['w_in']
    xr = z[..., :OFF_CQ]
    cq = z[..., OFF_CQ:OFF_CKV]
    ckv = z[..., OFF_CKV:OFF_KPE]
    kpe = z[..., OFF_KPE:OFF_U]
    u = z[..., OFF_U:OFF_V]
    v = z[..., OFF_V:OFF_G]
    g = z[..., OFF_G:]
    y_rg, h_last, conv_new = rglru_branch(xr, conv_prev, h0, pos, lp['conv_w'], lp['conv_b'],
                                          lp['lru_gate_a_w'], lp['lru_gate_a_b'],
                                          lp['lru_gate_x_w'], lp['lru_gate_x_b'], lp['lru_lambda'])
    q = (rms_norm(cq, lp['q_a_norm']) @ lp['w_uq']).reshape(bsz, t, MLA_HEADS, QK_HEAD)
    q_nope = rms_norm(q[..., :QK_NOPE], lp['q_nope_norm'])
    q_pe = rope(rms_norm(q[..., QK_NOPE:], lp['q_rope_norm']), pos)
    latent = rms_norm(ckv, lp['kv_a_norm'])
    k_pe = rope(rms_norm(kpe, lp['k_rope_norm']), pos)
    if paged is None:
        y_at = mla_prompt(q_nope, q_pe, latent, k_pe, lp['w_ukv'], lp['k_nope_norm'])
    else:
        cache_lat, cache_kpe, page_table, layer = paged
        y_at = mla_sample(q_nope, q_pe, latent, k_pe, cache_lat, cache_kpe, page_table, layer,
                          lp['w_ukv'], lp['k_nope_norm'])
    y_cm, v_rows = sgu_branch(jax.nn.gelu(u), jax.nn.gelu(v), lp['sgu_ln_g'], lp['sgu_ln_b'],
                              lp['sgu_w'], lp['sgu_b'])
    ys = jnp.stack([y_rg, y_at, y_cm], axis=2)
    proj = jnp.einsum('btnc,ncd->btnd', ys, lp['w_branch'])
    gates = jax.nn.sigmoid(g.reshape(bsz, t, N_BRANCH, D_MODEL))
    x = x + jnp.sum(gates * proj, axis=2) @ lp['w_out']
    hf = rms_norm(x, lp['ffn_norm'])
    x = x + (jax.nn.silu(hf @ lp['w_gate']) * (hf @ lp['w_up'])) @ lp['w_down']
    return x, latent, k_pe, h_last, conv_new, v_rows


def setup_inputs(seed: int = 0) -> dict:
    key = jax.random.key(seed)
    ks = iter(jax.random.split(key, 48))
    f32 = jnp.float32

    def nrm(shape, scale):
        return jax.random.normal(next(ks), shape, f32) * scale

    def gain(shape):
        return 1.0 + nrm(shape, 0.05)

    n_pages = PAST_LEN // PAGE_SIZE
    n_used = DEC_BATCH * n_pages
    n_phys = n_used + max(1, n_used // 4)
    perm = jax.random.permutation(next(ks), n_phys)
    page_table = perm[:n_used].reshape(DEC_BATCH, n_pages).astype(jnp.int32)
    a0 = jax.random.uniform(next(ks), (DEPTH, LRU_WIDTH), f32, 0.9, 0.999) ** (1.0 / LRU_C)
    lru_lambda = jnp.log(a0) - jnp.log1p(-a0)
    return {
        'x_prompt': nrm((BATCH, SEQ, D_MODEL), 1.0),
        'x_sample': nrm((DEC_BATCH, DEC_SEQ, D_MODEL), 1.0),
        'cache_kv_latent': nrm((DEPTH, n_phys, PAGE_SIZE, KV_LORA), 1.0),
        'cache_k_rope': nrm((DEPTH, n_phys, PAGE_SIZE, QK_ROPE), 1.0),
        'page_table': page_table,
        'state_rglru_h': nrm((DEPTH, DEC_BATCH, LRU_WIDTH), 0.5),
        'state_rglru_conv': nrm((DEPTH, DEC_BATCH, CONV_WIDTH - 1, LRU_WIDTH), 1.0),
        'attn_norm': gain((DEPTH, D_MODEL)),
        'w_in': nrm((DEPTH, D_MODEL, IN_WIDTH), D_MODEL ** -0.5),
        'conv_w': nrm((DEPTH, CONV_WIDTH, LRU_WIDTH), CONV_WIDTH ** -0.5),
        'conv_b': nrm((DEPTH, LRU_WIDTH), 0.01),
        'lru_gate_a_w': nrm((DEPTH, LRU_BLOCKS, LRU_BLOCK, LRU_BLOCK), LRU_BLOCK ** -0.5),
        'lru_gate_a_b': nrm((DEPTH, LRU_BLOCKS, LRU_BLOCK), 0.01),
        'lru_gate_x_w': nrm((DEPTH, LRU_BLOCKS, LRU_BLOCK, LRU_BLOCK), LRU_BLOCK ** -0.5),
        'lru_gate_x_b': nrm((DEPTH, LRU_BLOCKS, LRU_BLOCK), 0.01),
        'lru_lambda': lru_lambda,
        'q_a_norm': gain((DEPTH, Q_LORA)),
        'w_uq': nrm((DEPTH, Q_LORA, MLA_HEADS * QK_HEAD), Q_LORA ** -0.5),
        'kv_a_norm': gain((DEPTH, KV_LORA)),
        'w_ukv': nrm((DEPTH, KV_LORA, MLA_HEADS * (QK_NOPE + V_HEAD)), KV_LORA ** -0.5),
        'q_nope_norm': gain((DEPTH, QK_NOPE)),
        'q_rope_norm': gain((DEPTH, QK_ROPE)),
        'k_nope_norm': gain((DEPTH, QK_NOPE)),
        'k_rope_norm': gain((DEPTH, QK_ROPE)),
        'sgu_ln_g': gain((DEPTH, SGU_WIDTH)),
        'sgu_ln_b': nrm((DEPTH, SGU_WIDTH), 0.01),
        'sgu_w': nrm((DEPTH, SGU_GROUPS, CHUNK, CHUNK), CHUNK ** -0.5),
        'sgu_b': gain((DEPTH, SGU_GROUPS, CHUNK)),
        'w_branch': nrm((DEPTH, N_BRANCH, BRANCH_WIDTH, D_MODEL), BRANCH_WIDTH ** -0.5),
        'w_out': nrm((DEPTH, D_MODEL, D_MODEL), D_MODEL ** -0.5),
        'ffn_norm': gain((DEPTH, D_MODEL)),
        'w_gate': nrm((DEPTH, D_MODEL, D_FF), D_MODEL ** -0.5),
        'w_up': nrm((DEPTH, D_MODEL, D_FF), D_MODEL ** -0.5),
        'w_down': nrm((DEPTH, D_FF, D_MODEL), D_FF ** -0.5),
    }


def reference(x_prompt, x_sample, cache_kv_latent, cache_k_rope, page_table, state_rglru_h, state_rglru_conv,
              attn_norm, w_in, conv_w, conv_b, lru_gate_a_w, lru_gate_a_b, lru_gate_x_w, lru_gate_x_b, lru_lambda,
              q_a_norm, w_uq, kv_a_norm, w_ukv, q_nope_norm, q_rope_norm, k_nope_norm, k_rope_norm,
              sgu_ln_g, sgu_ln_b, sgu_w, sgu_b, w_branch, w_out, ffn_norm, w_gate, w_up, w_down):
    bsz_p, seq_p, _ = x_prompt.shape
    bsz_s, seq_s, _ = x_sample.shape
    past = page_table.shape[1] * PAGE_SIZE
    pos_p = jnp.arange(seq_p)
    pos_s = past + jnp.arange(seq_s)
    xp, xs = x_prompt, x_sample
    lat_p, kpe_p, hp_l, cp_l = [], [], [], []
    lat_s, kpe_s, hs_l, cs_l, vs_l = [], [], [], [], []
    for l in range(DEPTH):
        lp = dict(attn_norm=attn_norm[l], w_in=w_in[l], conv_w=conv_w[l], conv_b=conv_b[l],
                  lru_gate_a_w=lru_gate_a_w[l], lru_gate_a_b=lru_gate_a_b[l],
                  lru_gate_x_w=lru_gate_x_w[l], lru_gate_x_b=lru_gate_x_b[l], lru_lambda=lru_lambda[l],
                  q_a_norm=q_a_norm[l], w_uq=w_uq[l], kv_a_norm=kv_a_norm[l], w_ukv=w_ukv[l],
                  q_nope_norm=q_nope_norm[l], q_rope_norm=q_rope_norm[l],
                  k_nope_norm=k_nope_norm[l], k_rope_norm=k_rope_norm[l],
                  sgu_ln_g=sgu_ln_g[l], sgu_ln_b=sgu_ln_b[l], sgu_w=sgu_w[l], sgu_b=sgu_b[l],
                  w_branch=w_branch[l], w_out=w_out[l], ffn_norm=ffn_norm[l],
                  w_gate=w_gate[l], w_up=w_up[l], w_down=w_down[l])
        conv0 = jnp.zeros((bsz_p, CONV_WIDTH - 1, LRU_WIDTH), xp.dtype)
        h00 = jnp.zeros((bsz_p, LRU_WIDTH), xp.dtype)
        xp, lat, kpe, hl, cn, _ = trunk_layer(xp, pos_p, conv0, h00, lp)
        lat_p.append(lat); kpe_p.append(kpe); hp_l.append(hl); cp_l.append(cn)
        xs, lat, kpe, hl, cn, vr = trunk_layer(xs, pos_s, state_rglru_conv[l], state_rglru_h[l], lp,
                                               paged=(cache_kv_latent, cache_k_rope, page_table, l))
        lat_s.append(lat); kpe_s.append(kpe); hs_l.append(hl); cs_l.append(cn); vs_l.append(vr)
    return (xp, xs,
            jnp.stack(lat_p), jnp.stack(kpe_p), jnp.stack(hp_l), jnp.stack(cp_l),
            jnp.stack(lat_s), jnp.stack(kpe_s), jnp.stack(hs_l), jnp.stack(cs_l), jnp.stack(vs_l))
```

```python
import functools

import jax
import jax.numpy as jnp
import numpy as np
from jax import lax
from jax.experimental import pallas as pl
from jax.experimental.pallas import tpu as pltpu

HEADS = 8
QK_NOPE = 64
QK_ROPE = 32
ROPE_HALF = QK_ROPE // 2
QK_HEAD = QK_NOPE + QK_ROPE
V_HEAD = 64
HEAD_BLOCK = 128
LRU_BLOCKS = 8
CONV_WIDTH = 4
LRU_C = 8.0
SGU_GROUPS = 4
CHUNK = 128
PAGE_SIZE = 128
ROPE_THETA = 10000.0
ATTN_SCALE = QK_HEAD ** -0.5
EPS = 1e-6
NEG_INF = -1e30

LANES = 128
SUBLANES = 8
VMEM_LIMIT_BYTES = 56 * 1024 * 1024

BF16 = jnp.bfloat16
F32 = jnp.float32


def _dot(a, b):
    return jnp.dot(a, b, preferred_element_type=F32)


def _dot_nt(a, b):
    return lax.dot_general(a, b, (((1,), (1,)), ((), ())), preferred_element_type=F32)


def _dot_tn(a, b):
    return lax.dot_general(a, b, (((0,), (0,)), ((), ())), preferred_element_type=F32)


def _sigmoid(x):
    return 1.0 / (1.0 + jnp.exp(-x))


def _gelu_tanh(x):
    return 0.5 * x * (1.0 + jnp.tanh(np.sqrt(2.0 / np.pi).astype(np.float32) * (x + 0.044715 * (x * x * x))))


def _rms_rows(x):
    return x * lax.rsqrt(jnp.mean(x * x, axis=-1, keepdims=True) + EPS)


def _full_spec(shape):
    nd = len(shape)
    return pl.BlockSpec(shape, lambda *_: (0,) * nd)


def _compiler_params(semantics):
    return pltpu.CompilerParams(dimension_semantics=semantics, vmem_limit_bytes=VMEM_LIMIT_BYTES)


def _proj_kernel(x_ref, ca_ref, sb_ref, an_ref, wmix_ref, qan_ref, wuq_ref, kvan_ref, wukv_ref,
                 indq_ref, indk_ref, gqa_ref, gqb_ref, gka_ref, gkb_ref, gkn_ref, lng_ref, lnb_ref,
                 xr_ref, q_ref, k_ref, v_ref, lat_ref, kpe_ref, ug_ref, vln_ref):
    width = xr_ref.shape[-1]
    lora = lat_ref.shape[-1]
    x = x_ref[...]
    h = (_rms_rows(x) * an_ref[...]).astype(BF16)
    z = _dot(h, wmix_ref[...])
    o = 0
    xr_ref[...] = z[:, o:o + width]
    o += width
    cq = z[:, o:o + lora]
    o += lora
    ckv = z[:, o:o + lora]
    o += lora
    zu = z[:, o:o + width]
    o += width
    zv = z[:, o:o + width]
    o += width
    zka = z[:, o:o + HEAD_BLOCK]
    o += HEAD_BLOCK
    zkb = z[:, o:o + HEAD_BLOCK]

    ca = ca_ref[...]
    sb = sb_ref[...]

    cqn = (_rms_rows(cq) * qan_ref[...]).astype(BF16)
    qab = _dot(cqn, wuq_ref[...])
    ta = ca * gqa_ref[...]
    tb = sb * gqb_ref[...]
    pair = 2 * HEAD_BLOCK
    for p in range(HEADS // 2):
        a = qab[:, p * pair:(p + 1) * pair]
        b = qab[:, HEADS * HEAD_BLOCK + p * pair:HEADS * HEAD_BLOCK + (p + 1) * pair]
        nrm = lax.rsqrt(_dot((a * a).astype(BF16), indq_ref[...]) + EPS)
        for e in range(2):
            sl = slice(e * HEAD_BLOCK, (e + 1) * HEAD_BLOCK)
            q_ref[:, p * pair + e * HEAD_BLOCK:p * pair + (e + 1) * HEAD_BLOCK] = (
                (a[:, sl] * ta + b[:, sl] * tb) * nrm[:, sl]).astype(q_ref.dtype)

    lat = _rms_rows(ckv) * kvan_ref[...]
    lat_ref[...] = lat
    kv = _dot(lat.astype(BF16), wukv_ref[...])
    v_ref[...] = kv[:, HEADS * HEAD_BLOCK:].astype(v_ref.dtype)

    nk = lax.rsqrt(jnp.sum(zka * zka, axis=-1, keepdims=True) * (1.0 / QK_ROPE) + EPS)
    kpe = (zka * (ca * gka_ref[...]) + zkb * (sb * gkb_ref[...])) * nk
    kpe_ref[...] = kpe

    gkn = gkn_ref[...]
    for p in range(HEADS // 2):
        a = kv[:, p * pair:(p + 1) * pair]
        nrm = lax.rsqrt(_dot((a * a).astype(BF16), indk_ref[...]) + EPS)
        for e in range(2):
            sl = slice(e * HEAD_BLOCK, (e + 1) * HEAD_BLOCK)
            k_ref[:, p * pair + e * HEAD_BLOCK:p * pair + (e + 1) * HEAD_BLOCK] = (
                a[:, sl] * nrm[:, sl] * gkn + kpe).astype(k_ref.dtype)

    ug_ref[...] = _gelu_tanh(zu).astype(ug_ref.dtype)
    gv = _gelu_tanh(zv)
    mu = jnp.mean(gv, axis=-1, keepdims=True)
    gc = gv - mu
    var = jnp.mean(gc * gc, axis=-1, keepdims=True)
    vln_ref[...] = gc * lax.rsqrt(var + EPS) * lng_ref[...] + lnb_ref[...]


def _proj_call(x, ca, sb, lw, tm, q_dtype):
    n, d = x.shape
    width = lw['lng'].shape[-1]
    lora = lw['qan'].shape[-1]
    hw = HEADS * HEAD_BLOCK
    weights = [lw['an'], lw['wmix'], lw['qan'], lw['wuq'], lw['kvan'], lw['wukv'], lw['indq'], lw['indk'],
               lw['gqa'], lw['gqb'], lw['gka'], lw['gkb'], lw['gkn'], lw['lng'], lw['lnb']]
    n_tab = ca.shape[0] // tm

    def row(i):
        return (i, 0)

    def tab(i):
        return (i % n_tab, 0)

    in_specs = ([pl.BlockSpec((tm, d), row), pl.BlockSpec((tm, HEAD_BLOCK), tab), pl.BlockSpec((tm, HEAD_BLOCK), tab)]
                + [_full_spec(w.shape) for w in weights])
    out_widths = [(width, F32), (hw, q_dtype), (hw, BF16), (hw, BF16), (lora, F32), (HEAD_BLOCK, F32),
                  (width, BF16), (width, F32)]
    return pl.pallas_call(
        _proj_kernel,
        grid=(n // tm,),
        in_specs=in_specs,
        out_specs=[pl.BlockSpec((tm, w), row) for w, _ in out_widths],
        out_shape=[jax.ShapeDtypeStruct((n, w), dt) for w, dt in out_widths],
        compiler_params=_compiler_params(("parallel",)),
        name="proj",
    )(x, ca, sb, *weights)


def _rglru_kernel(xr_ref, tail0_ref, h0_ref, cw_ref, cb_ref, wg_ref, bg_ref, sp_ref,
                  y_ref, hlast_ref, tail_s, h_s, a_s, b_s, hs_s, *, first_pos_is_zero):
    c = pl.program_id(0)
    bsz, tc, width = xr_ref.shape

    @pl.when(c == 0)
    def _():
        tail_s[...] = tail0_ref[...]
        h_s[...] = jnp.broadcast_to(h0_ref[...], h_s.shape)

    x = xr_ref[...]
    xp = jnp.concatenate([tail_s[...], x], axis=1)
    cw = cw_ref[...]
    xc = cb_ref[...] + x * cw[CONV_WIDTH - 1:CONV_WIDTH]
    for k in range(CONV_WIDTH - 1):
        shift = CONV_WIDTH - 1 - k
        xc = xc + xp[:, SUBLANES - shift:SUBLANES - shift + tc] * cw[k:k + 1]
    tail_s[...] = x[:, tc - SUBLANES:, :]

    xc2 = xc.reshape(bsz * tc, width)
    gates = _dot(xc2.astype(BF16), wg_ref[...]) + bg_ref[...]
    r = _sigmoid(gates[:, :width])
    i = _sigmoid(gates[:, width:])
    log_a = (-LRU_C) * r * sp_ref[...]
    a = jnp.exp(log_a)
    mult = jnp.sqrt(-jnp.tanh(log_a) * (a * a + 1.0))
    b = mult * (i * xc2)
    a_s[...] = a.reshape(bsz, tc, width)
    b_s[...] = b.reshape(bsz, tc, width)

    if first_pos_is_zero:
        @pl.when(c == 0)
        def _():
            b_s[:, 0:1, :] = (i * xc2).reshape(bsz, tc, width)[:, 0:1, :]

    row = lax.broadcasted_iota(jnp.int32, (bsz, SUBLANES, width), 1)

    def group(g, h):
        start = pl.multiple_of(g * SUBLANES, SUBLANES)
        av = a_s[:, pl.ds(start, SUBLANES), :]
        bv = b_s[:, pl.ds(start, SUBLANES), :]
        for dlt in (1, 2, 4):
            keep = row >= dlt
            a_sh = jnp.where(keep, pltpu.roll(av, dlt, axis=1), 1.0)
            b_sh = jnp.where(keep, pltpu.roll(bv, dlt, axis=1), 0.0)
            bv = av * b_sh + bv
            av = av * a_sh
        hall = av * h + bv
        hs_s[:, pl.ds(start, SUBLANES), :] = hall
        return jnp.broadcast_to(hall[:, SUBLANES - 1:SUBLANES, :], hall.shape)

    h = lax.fori_loop(0, tc // SUBLANES, group, h_s[...])
    h_s[...] = h
    y_ref[...] = hs_s[...].astype(y_ref.dtype)
    hlast_ref[...] = h[:, 0:1, :]


def _rglru_call(xr, tail0, h0, lw, tc, first_pos_is_zero):
    bsz, t, width = xr.shape
    weights = [lw['cw'], lw['cb'], lw['wgate'], lw['bgate'], lw['sp']]
    kern = functools.partial(_rglru_kernel, first_pos_is_zero=first_pos_is_zero)
    return pl.pallas_call(
        kern,
        grid=(t // tc,),
        in_specs=[pl.BlockSpec((bsz, tc, width), lambda c: (0, c, 0)),
                  _full_spec(tail0.shape), _full_spec(h0.shape)] + [_full_spec(w.shape) for w in weights],
        out_specs=[pl.BlockSpec((bsz, tc, width), lambda c: (0, c, 0)),
                   pl.BlockSpec((bsz, 1, width), lambda c: (0, 0, 0))],
        out_shape=[jax.ShapeDtypeStruct((bsz, t, width), BF16), jax.ShapeDtypeStruct((bsz, 1, width), F32)],
        scratch_shapes=[pltpu.VMEM((bsz, SUBLANES, width), F32), pltpu.VMEM((bsz, SUBLANES, width), F32),
                        pltpu.VMEM((bsz, tc, width), F32), pltpu.VMEM((bsz, tc, width), F32),
                        pltpu.VMEM((bsz, tc, width), F32)],
        compiler_params=_compiler_params(("arbitrary",)),
        name="rglru",
    )(xr, tail0, h0, *weights)


def _attn_prompt_kernel(q_ref, k_ref, v_ref, o_ref, *, tk):
    qi = pl.program_id(1)
    tq = q_ref.shape[0]
    rows = lax.broadcasted_iota(jnp.int32, (tq, tk), 0)
    cols = lax.broadcasted_iota(jnp.int32, (tq, tk), 1)
    diag_mask = cols <= rows

    def head(hh):
        lanes = slice(hh * HEAD_BLOCK, (hh + 1) * HEAD_BLOCK)
        q = q_ref[:, lanes]

        def step(j, carry, masked):
            m, l, acc = carry
            start = pl.multiple_of(j * tk, tk)
            k = k_ref[pl.ds(start, tk), lanes]
            v = v_ref[pl.ds(start, tk), lanes]
            s = _dot_nt(q, k)
            if masked:
                s = jnp.where(diag_mask, s, NEG_INF)
            m_new = jnp.maximum(m, jnp.max(s, axis=-1, keepdims=True))
            p = jnp.exp(s - m_new)
            alpha = jnp.exp(m - m_new)
            l = alpha * l + jnp.sum(p, axis=-1, keepdims=True)
            acc = alpha * acc + _dot(p.astype(BF16), v)
            return m_new, l, acc

        init = (jnp.full((tq, 1), NEG_INF, F32), jnp.zeros((tq, 1), F32), jnp.zeros((tq, HEAD_BLOCK), F32))
        carry = lax.fori_loop(0, qi, lambda j, cr: step(j, cr, False), init)
        _, l, acc = step(qi, carry, True)
        return acc / l

    for p in range(HEADS // 2):
        o_ref[:, p * HEAD_BLOCK:(p + 1) * HEAD_BLOCK] = (head(2 * p) + head(2 * p + 1)).astype(o_ref.dtype)


def _attn_prompt_call(q, k, v, bsz, t, tq):
    n, hw = q.shape
    nq = t // tq
    kern = functools.partial(_attn_prompt_kernel, tk=tq)
    return pl.pallas_call(
        kern,
        grid=(bsz, nq),
        in_specs=[pl.BlockSpec((tq, hw), lambda b, i: (b * nq + i, 0)),
                  pl.BlockSpec((t, hw), lambda b, i: (b, 0)),
                  pl.BlockSpec((t, hw), lambda b, i: (b, 0))],
        out_specs=pl.BlockSpec((tq, HEADS * V_HEAD), lambda b, i: (b * nq + i, 0)),
        out_shape=jax.ShapeDtypeStruct((n, HEADS * V_HEAD), BF16),
        compiler_params=_compiler_params(("parallel", "arbitrary")),
        name="attn_prompt",
    )(q, k, v)


def _attn_sample_kernel(pt_ref, qbd_ref, qp_ref, latn_ref, kpen_ref, wuk_ref, ind_ref, wuv_ref,
                        lat_hbm, kpe_hbm, o_ref, lat_buf, kpe_buf, sem, m_s, l_s, acc_s,
                        *, layer, pages_per_step, n_steps):
    b = pl.program_id(0)
    c = pl.program_id(1)
    nb = pl.num_programs(0)
    step = b * n_steps + c
    slot = step % 2
    ck = pages_per_step * PAGE_SIZE
    hq = qbd_ref.shape[-1]
    t_new = hq // HEADS

    def copies(bb, cc, sl):
        out = []
        for i in range(pages_per_step):
            page = pt_ref[bb, cc * pages_per_step + i]
            out.append(pltpu.make_async_copy(lat_hbm.at[layer, page],
                                             lat_buf.at[sl, pl.ds(i * PAGE_SIZE, PAGE_SIZE)], sem.at[0, sl]))
            out.append(pltpu.make_async_copy(kpe_hbm.at[layer, page],
                                             kpe_buf.at[sl, pl.ds(i * PAGE_SIZE, PAGE_SIZE)], sem.at[1, sl]))
        return out

    @pl.when(step == 0)
    def _():
        for cp in copies(0, 0, 0):
            cp.start()

    nxt = step + 1

    @pl.when(nxt < nb * n_steps)
    def _():
        for cp in copies(nxt // n_steps, nxt % n_steps, nxt % 2):
            cp.start()

    @pl.when(c == 0)
    def _():
        m_s[...] = jnp.full(m_s.shape, NEG_INF, F32)
        l_s[...] = jnp.zeros(l_s.shape, F32)
        acc_s[...] = jnp.zeros(acc_s.shape, F32)

    for cp in copies(b, c, slot):
        cp.wait()

    qbd = qbd_ref[0]
    qp = qp_ref[0]

    def update(lat, kpe, mask):
        latb = lat.astype(BF16)
        kx = _dot(latb, wuk_ref[...])
        ms = _dot((kx * kx).astype(BF16), ind_ref[...])
        s = _dot(kx.astype(BF16), qbd) * lax.rsqrt(ms + EPS) + _dot(kpe.astype(BF16), qp)
        if mask is not None:
            s = jnp.where(mask, s, NEG_INF)
        m_old = m_s[...]
        m_new = jnp.maximum(m_old, jnp.max(s, axis=0, keepdims=True))
        p = jnp.exp(s - m_new)
        alpha = jnp.exp(m_old - m_new)
        l_s[...] = alpha * l_s[...] + jnp.sum(p, axis=0, keepdims=True)
        m_s[...] = m_new
        pv = _dot_tn(p.astype(BF16), latb)
        acc_s[...] = acc_s[...] * jnp.transpose(alpha) + pv

    update(lat_buf[slot], kpe_buf[slot], None)

    @pl.when(c == n_steps - 1)
    def _():
        key = lax.broadcasted_iota(jnp.int32, (t_new, hq), 0)
        qry = lax.broadcasted_iota(jnp.int32, (t_new, hq), 1) % t_new
        update(latn_ref[0], kpen_ref[0], key <= qry)
        ol = (acc_s[...] / jnp.transpose(l_s[...])).astype(BF16)
        for hh in range(HEADS):
            o_ref[0, :, hh * V_HEAD:(hh + 1) * V_HEAD] = _dot(
                ol[hh * t_new:(hh + 1) * t_new], wuv_ref[hh]).astype(o_ref.dtype)


def _attn_sample_call(page_table, qbd, qp, lat_new, kpe_new, lw, cache_lat, cache_kpe, layer, pages_per_step):
    bsz, n_pages = page_table.shape
    n_steps = n_pages // pages_per_step
    ck = pages_per_step * PAGE_SIZE
    lora = lat_new.shape[-1]
    t_new = lat_new.shape[1]
    hq = qbd.shape[-1]
    weights = [lw['wuk_s'], lw['ind_s'], lw['wuv_s']]
    kern = functools.partial(_attn_sample_kernel, layer=layer, pages_per_step=pages_per_step, n_steps=n_steps)
    grid_spec = pltpu.PrefetchScalarGridSpec(
        num_scalar_prefetch=1,
        grid=(bsz, n_steps),
        in_specs=[pl.BlockSpec((1,) + qbd.shape[1:], lambda b, c, pt: (b, 0, 0)),
                  pl.BlockSpec((1,) + qp.shape[1:], lambda b, c, pt: (b, 0, 0)),
                  pl.BlockSpec((1, t_new, lora), lambda b, c, pt: (b, 0, 0)),
                  pl.BlockSpec((1, t_new, QK_ROPE), lambda b, c, pt: (b, 0, 0))]
        + [pl.BlockSpec(w.shape, lambda b, c, pt, nd=w.ndim: (0,) * nd) for w in weights]
        + [pl.BlockSpec(memory_space=pl.ANY), pl.BlockSpec(memory_space=pl.ANY)],
        out_specs=pl.BlockSpec((1, t_new, HEADS * V_HEAD), lambda b, c, pt: (b, 0, 0)),
        scratch_shapes=[pltpu.VMEM((2, ck, lora), F32), pltpu.VMEM((2, ck, QK_ROPE), F32),
                        pltpu.SemaphoreType.DMA((2, 2)),
                        pltpu.VMEM((1, hq), F32), pltpu.VMEM((1, hq), F32), pltpu.VMEM((hq, lora), F32)],
    )
    return pl.pallas_call(
        kern,
        grid_spec=grid_spec,
        out_shape=jax.ShapeDtypeStruct((bsz, t_new, HEADS * V_HEAD), BF16),
        compiler_params=_compiler_params(("arbitrary", "arbitrary")),
        name="attn_sample",
    )(page_table, qbd, qp, lat_new, kpe_new, *weights, cache_lat, cache_kpe)


def _merge_kernel(x_ref, yrg_ref, yat_ref, ug_ref, vln_ref, an_ref, wg_ref, wb_ref, wout_ref, sgw_ref, sgb_ref,
                  o_ref, ycm_s):
    tm, d = x_ref.shape
    period = sgw_ref.shape[-1]
    gdim = vln_ref.shape[-1] // SGU_GROUPS
    x = x_ref[...]
    h = (_rms_rows(x) * an_ref[...]).astype(BF16)
    gates = _dot(h, wg_ref[...])

    for cidx in range(tm // period):
        rows = slice(cidx * period, (cidx + 1) * period)
        for g in range(SGU_GROUPS):
            cols = slice(g * gdim, (g + 1) * gdim)
            s = _dot(sgw_ref[g], vln_ref[rows, cols].astype(BF16)) + sgb_ref[:, cols]
            ycm_s[rows, cols] = (ug_ref[rows, cols].astype(F32) * s).astype(BF16)

    merged = _sigmoid(gates[:, :d]) * _dot(yrg_ref[...], wb_ref[0])
    merged = merged + _sigmoid(gates[:, d:2 * d]) * _dot(yat_ref[...], wb_ref[1])
    merged = merged + _sigmoid(gates[:, 2 * d:]) * _dot(ycm_s[...], wb_ref[2])
    o_ref[...] = x + _dot(merged.astype(BF16), wout_ref[...])


def _merge_call(x, yrg, yat, ug, vln, lw, sgw, sgb, tm):
    n, d = x.shape
    width = yrg.shape[-1]
    weights = [lw['an'], lw['wgates'], lw['wbranch'], lw['wout'], sgw, sgb]

    def row(i):
        return (i, 0)

    return pl.pallas_call(
        _merge_kernel,
        grid=(n // tm,),
        in_specs=[pl.BlockSpec((tm, d), row)] + [pl.BlockSpec((tm, width), row)] * 4
        + [_full_spec(w.shape) for w in weights],
        out_specs=pl.BlockSpec((tm, d), row),
        out_shape=jax.ShapeDtypeStruct((n, d), F32),
        scratch_shapes=[pltpu.VMEM((tm, width), BF16)],
        compiler_params=_compiler_params(("parallel",)),
        name="merge",
    )(x, yrg, yat, ug, vln, *weights)


def _ffn_kernel(x_ref, fn_ref, wgu_ref, wd_ref, o_ref):
    dff = wd_ref.shape[0]
    x = x_ref[...]
    h = (_rms_rows(x) * fn_ref[...]).astype(BF16)
    gu = _dot(h, wgu_ref[...])
    gate = gu[:, :dff]
    act = (gate * _sigmoid(gate) * gu[:, dff:]).astype(BF16)
    o_ref[...] = x + _dot(act, wd_ref[...])


def _ffn_call(x, lw, tm):
    n, d = x.shape
    weights = [lw['fn'], lw['wgu'], lw['wdown']]

    def row(i):
        return (i, 0)

    return pl.pallas_call(
        _ffn_kernel,
        grid=(n // tm,),
        in_specs=[pl.BlockSpec((tm, d), row)] + [_full_spec(w.shape) for w in weights],
        out_specs=pl.BlockSpec((tm, d), row),
        out_shape=jax.ShapeDtypeStruct((n, d), F32),
        compiler_params=_compiler_params(("parallel",)),
        name="ffn",
    )(x, *weights)


def _head_blocks(parts):
    used = sum(p.shape[-1] for p in parts)
    pad = jnp.zeros(parts[0].shape[:-1] + (HEAD_BLOCK - used,), parts[0].dtype)
    blk = jnp.concatenate(list(parts) + [pad], axis=-1)
    return blk.reshape(blk.shape[:-2] + (HEADS * HEAD_BLOCK,))


def _block_vec(parts):
    used = sum(p.shape[-1] for p in parts)
    return jnp.concatenate(list(parts) + [jnp.zeros((HEAD_BLOCK - used,), F32)])[None, :]


def _pack_layer(l, p, dims):
    width, lora = dims['width'], dims['lora']
    w_in = p['w_in'][l]
    o_cq = width
    o_ckv = o_cq + lora
    o_kpe = o_ckv + lora
    o_u = o_kpe + QK_ROPE
    o_v = o_u + width
    o_g = o_v + width
    d = w_in.shape[0]
    wk = w_in[:, o_kpe:o_u]
    zn = jnp.zeros((d, QK_NOPE), F32)
    zr = jnp.zeros((d, HEAD_BLOCK - QK_HEAD), F32)
    k1, k2 = wk[:, :ROPE_HALF], wk[:, ROPE_HALF:]
    wmix = jnp.concatenate([w_in[:, :o_kpe], w_in[:, o_u:o_g], zn, k1, k2, zr, zn, k2, k1, zr], axis=-1)

    wuq = p['w_uq'][l].reshape(lora, HEADS, QK_HEAD)
    qn, q1, q2 = wuq[..., :QK_NOPE], wuq[..., QK_NOPE:QK_NOPE + ROPE_HALF], wuq[..., QK_NOPE + ROPE_HALF:]
    wuq_a = _head_blocks([qn, q1, q2])
    wuq_b = _head_blocks([jnp.zeros_like(qn), q2, q1])

    wukv = p['w_ukv'][l].reshape(lora, HEADS, QK_NOPE + V_HEAD)
    kn, vv = wukv[..., :QK_NOPE], wukv[..., QK_NOPE:]
    wuk_blocks = _head_blocks([kn])
    zv = jnp.zeros_like(vv)
    even = (jnp.arange(HEADS) % 2 == 0)[None, :, None]
    wuv_blocks = jnp.concatenate([jnp.where(even, vv, zv), jnp.where(even, zv, vv)], axis=-1)
    wuv_blocks = wuv_blocks.reshape(lora, HEADS * HEAD_BLOCK)

    lane = np.arange(2 * HEAD_BLOCK) % HEAD_BLOCK
    blk = np.arange(2 * HEAD_BLOCK) // HEAD_BLOCK
    same = blk[:, None] == blk[None, :]
    nope = lane < QK_NOPE
    ropel = (lane >= QK_NOPE) & (lane < QK_HEAD)
    ind_nope = (same & nope[:, None] & nope[None, :]) / QK_NOPE
    ind_rope = (same & ropel[:, None] & ropel[None, :]) / QK_ROPE

    gq_n, gq_r = p['q_nope_norm'][l] * ATTN_SCALE, p['q_rope_norm'][l] * ATTN_SCALE
    gk_r = p['k_rope_norm'][l]
    zero_n = jnp.zeros((QK_NOPE,), F32)

    nblk = LRU_BLOCKS
    bw = width // nblk
    eye = jnp.eye(nblk, dtype=F32)

    def block_diag(w):
        return jnp.einsum('ncd,nm->ncmd', w, eye).reshape(width, width)

    lam = p['lru_lambda'][l]
    sp = jnp.maximum(-lam, 0.0) + jnp.log1p(jnp.exp(-jnp.abs(lam)))

    t_new = dims['t_new']
    hq = HEADS * t_new
    hcol = np.arange(hq) // t_new
    hrow = np.arange(HEADS * QK_NOPE) // QK_NOPE
    ind_s = (hrow[:, None] == hcol[None, :]) / QK_NOPE

    return dict(
        an=p['attn_norm'][l][None, :],
        wmix=wmix.astype(BF16),
        wgates=w_in[:, o_g:].astype(BF16),
        qan=p['q_a_norm'][l][None, :],
        wuq=jnp.concatenate([wuq_a, wuq_b], axis=-1).astype(BF16),
        kvan=p['kv_a_norm'][l][None, :],
        wukv=jnp.concatenate([wuk_blocks, wuv_blocks], axis=-1).astype(BF16),
        indq=jnp.asarray(ind_nope + ind_rope, BF16),
        indk=jnp.asarray(ind_nope, BF16),
        gqa=_block_vec([gq_n, gq_r]),
        gqb=_block_vec([zero_n, gq_r[ROPE_HALF:], gq_r[:ROPE_HALF]]),
        gka=_block_vec([zero_n, gk_r]),
        gkb=_block_vec([zero_n, gk_r[ROPE_HALF:], gk_r[:ROPE_HALF]]),
        gkn=_block_vec([p['k_nope_norm'][l]]),
        lng=p['sgu_ln_g'][l][None, :],
        lnb=p['sgu_ln_b'][l][None, :],
        cw=p['conv_w'][l],
        cb=p['conv_b'][l][None, :],
        wgate=jnp.concatenate([block_diag(p['lru_gate_a_w'][l]), block_diag(p['lru_gate_x_w'][l])],
                              axis=-1).astype(BF16),
        bgate=jnp.concatenate([p['lru_gate_a_b'][l].reshape(1, width), p['lru_gate_x_b'][l].reshape(1, width)],
                              axis=-1),
        sp=sp[None, :],
        wbranch=p['w_branch'][l].astype(BF16),
        wout=p['w_out'][l].astype(BF16),
        fn=p['ffn_norm'][l][None, :],
        wgu=jnp.concatenate([p['w_gate'][l], p['w_up'][l]], axis=-1).astype(BF16),
        wdown=p['w_down'][l].astype(BF16),
        wuk_s=kn.reshape(lora, HEADS * QK_NOPE).astype(BF16),
        ind_s=jnp.asarray(ind_s, BF16),
        wuv_s=jnp.transpose(vv, (1, 0, 2)).astype(BF16),
        gk_n=p['k_nope_norm'][l],
    )


def _sgu_matrices(sgu_w, sgu_b, period, chunk_len):
    groups = sgu_w.shape[0]
    reps = period // chunk_len
    tri = jnp.tril(jnp.ones((chunk_len, chunk_len), F32))
    w = sgu_w[:, :chunk_len, :chunk_len] * tri
    mat = jnp.einsum('gts,rq->grtqs', w, jnp.eye(reps, dtype=F32)).reshape(groups, period, period)
    bias = jnp.tile(sgu_b[:, :chunk_len].T, (reps, 1))
    return mat.astype(BF16), bias


def _rope_tables(pos):
    inv = ROPE_THETA ** (-jnp.arange(ROPE_HALF, dtype=F32) / ROPE_HALF)
    ang = pos.astype(F32)[:, None] * inv[None, :]
    cos, sin = jnp.cos(ang), jnp.sin(ang)
    n = pos.shape[0]
    ones = jnp.ones((n, QK_NOPE), F32)
    zn = jnp.zeros((n, QK_NOPE), F32)
    zr = jnp.zeros((n, HEAD_BLOCK - QK_HEAD), F32)
    return (jnp.concatenate([ones, cos, cos, zr], axis=-1), jnp.concatenate([zn, -sin, sin, zr], axis=-1))


def _tile_rows(n, target):
    t = min(n, target)
    while n % t:
        t //= 2
    return t


def kernel(x_prompt, x_sample, cache_kv_latent, cache_k_rope, page_table, state_rglru_h, state_rglru_conv,
           attn_norm, w_in, conv_w, conv_b, lru_gate_a_w, lru_gate_a_b, lru_gate_x_w, lru_gate_x_b, lru_lambda,
           q_a_norm, w_uq, kv_a_norm, w_ukv, q_nope_norm, q_rope_norm, k_nope_norm, k_rope_norm,
           sgu_ln_g, sgu_ln_b, sgu_w, sgu_b, w_branch, w_out, ffn_norm, w_gate, w_up, w_down):
    params = dict(attn_norm=attn_norm, w_in=w_in, conv_w=conv_w, conv_b=conv_b, lru_gate_a_w=lru_gate_a_w,
                  lru_gate_a_b=lru_gate_a_b, lru_gate_x_w=lru_gate_x_w, lru_gate_x_b=lru_gate_x_b,
                  lru_lambda=lru_lambda, q_a_norm=q_a_norm, w_uq=w_uq, kv_a_norm=kv_a_norm, w_ukv=w_ukv,
                  q_nope_norm=q_nope_norm, q_rope_norm=q_rope_norm, k_nope_norm=k_nope_norm,
                  k_rope_norm=k_rope_norm, sgu_ln_g=sgu_ln_g, sgu_ln_b=sgu_ln_b, w_branch=w_branch, w_out=w_out,
                  ffn_norm=ffn_norm, w_gate=w_gate, w_up=w_up, w_down=w_down)
    depth = w_in.shape[0]
    bp, tp, d = x_prompt.shape
    bs, ts, _ = x_sample.shape
    width = conv_w.shape[-1]
    lora = q_a_norm.shape[-1]
    n_pages = page_table.shape[1]
    past = n_pages * PAGE_SIZE
    assert tp % CHUNK == 0 and ts <= CHUNK and past % CHUNK == 0 and ts == SUBLANES
    dims = dict(width=width, lora=lora, t_new=ts)
    n_p, n_s = bp * tp, bs * ts

    tm_p = _tile_rows(tp, 256)
    tm_s = n_s
    tq = _tile_rows(tp, 256)
    tc_p = _tile_rows(tp, 128)
    pages_per_step = _tile_rows(n_pages, 16)

    ca_p, sb_p = _rope_tables(jnp.arange(tp))
    ca_s, sb_s = _rope_tables(past + jnp.arange(ts))
    ca_s, sb_s = jnp.tile(ca_s, (bs, 1)), jnp.tile(sb_s, (bs, 1))

    xp = x_prompt.reshape(n_p, d)
    xs = x_sample.reshape(n_s, d)
    outs = {k: [] for k in ('lat_p', 'kpe_p', 'h_p', 'conv_p', 'lat_s', 'kpe_s', 'h_s', 'conv_s', 'v_s')}
    rope_lanes = slice(QK_NOPE, QK_HEAD)
    eye_h = jnp.eye(HEADS, dtype=F32)

    for l in range(depth):
        lw = _pack_layer(l, params, dims)
        sgw_p, sgb_p = _sgu_matrices(sgu_w[l], sgu_b[l], CHUNK, CHUNK)
        sgw_s, sgb_s = _sgu_matrices(sgu_w[l], sgu_b[l], tm_s, ts)
        sgb_p = jnp.repeat(sgb_p, width // SGU_GROUPS, axis=1)
        sgb_s = jnp.repeat(sgb_s, width // SGU_GROUPS, axis=1)

        xr, q, k, v, lat, kpe, ug, vln = _proj_call(xp, ca_p, sb_p, lw, tm_p, BF16)
        xr3 = xr.reshape(bp, tp, width)
        yrg, hlast = _rglru_call(xr3, jnp.zeros((bp, SUBLANES, width), F32), jnp.zeros((bp, 1, width), F32),
                                 lw, tc_p, True)
        yat = _attn_prompt_call(q, k, v, bp, tp, tq)
        x1 = _merge_call(xp, yrg.reshape(n_p, width), yat, ug, vln, lw, sgw_p, sgb_p, tm_p)
        xp = _ffn_call(x1, lw, tm_p)
        outs['lat_p'].append(lat.reshape(bp, tp, lora))
        outs['kpe_p'].append(kpe[:, rope_lanes].reshape(bp, tp, QK_ROPE))
        outs['h_p'].append(hlast.reshape(bp, width))
        outs['conv_p'].append(xr3[:, tp - (CONV_WIDTH - 1):, :])

        xr, q, _, _, lat, kpe, ug, vln = _proj_call(xs, ca_s, sb_s, lw, tm_s, F32)
        xr3 = xr.reshape(bs, ts, width)
        tail0 = jnp.concatenate([jnp.zeros((bs, SUBLANES - (CONV_WIDTH - 1), width), F32), state_rglru_conv[l]],
                                axis=1)
        yrg, hlast = _rglru_call(xr3, tail0, state_rglru_h[l][:, None, :], lw, ts, past == 0)
        q4 = q.reshape(bs, ts, HEADS, HEAD_BLOCK)
        qn = q4[..., :QK_NOPE] * lw['gk_n']
        qbd = jnp.einsum('bthd,hg->bhdgt', qn, eye_h).reshape(bs, HEADS * QK_NOPE, HEADS * ts).astype(BF16)
        qp = jnp.transpose(q4[..., rope_lanes], (0, 3, 2, 1)).reshape(bs, QK_ROPE, HEADS * ts).astype(BF16)
        lat3 = lat.reshape(bs, ts, lora)
        kpe3 = kpe[:, rope_lanes].reshape(bs, ts, QK_ROPE)
        yat = _attn_sample_call(page_table, qbd, qp, lat3, kpe3, lw, cache_kv_latent, cache_k_rope, l,
                                pages_per_step)
        x1 = _merge_call(xs, yrg.reshape(n_s, width), yat.reshape(n_s, width), ug, vln, lw, sgw_s, sgb_s, tm_s)
        xs = _ffn_call(x1, lw, tm_s)
        outs['lat_s'].append(lat3)
        outs['kpe_s'].append(kpe3)
        outs['h_s'].append(hlast.reshape(bs, width))
        outs['conv_s'].append(xr3[:, ts - (CONV_WIDTH - 1):, :])
        outs['v_s'].append(vln.reshape(bs, ts, width))

    st = {k: jnp.stack(v) for k, v in outs.items()}
    return (xp.reshape(bp, tp, d), xs.reshape(bs, ts, d),
            st['lat_p'], st['kpe_p'], st['h_p'], st['conv_p'],
            st['lat_s'], st['kpe_s'], st['h_s'], st['conv_s'], st['v_s'])
```

```python
import functools

import jax
import jax.numpy as jnp
import numpy as np
from jax import lax
from jax.experimental import pallas as pl
from jax.experimental.pallas import tpu as pltpu

HEADS = 8
QK_NOPE = 64
QK_ROPE = 32
ROPE_HALF = QK_ROPE // 2
QK_HEAD = QK_NOPE + QK_ROPE
V_HEAD = 64
HEAD_BLOCK = 128
LRU_BLOCKS = 8
CONV_WIDTH = 4
LRU_C = 8.0
SGU_GROUPS = 4
CHUNK = 128
PAGE_SIZE = 128
ROPE_THETA = 10000.0
ATTN_SCALE = QK_HEAD ** -0.5
LOG2_E = float(np.log2(np.e))
EPS = 1e-6
NEG_INF = -1e30
ATTN_LOOKAHEAD = 2
SAMPLE_CHUNK_PAGES = 16
SAMPLE_CHUNKS_PER_STEP = 4
SAMPLE_RING_SLOTS = 4

LANES = 128
SUBLANES = 8
VMEM_LIMIT_BYTES = 56 * 1024 * 1024

BF16 = jnp.bfloat16
F32 = jnp.float32


def _dot(a, b):
    return jnp.dot(a, b, preferred_element_type=F32)


def _dot_nt(a, b):
    return lax.dot_general(a, b, (((1,), (1,)), ((), ())), preferred_element_type=F32)


def _dot_tn(a, b):
    return lax.dot_general(a, b, (((0,), (0,)), ((), ())), preferred_element_type=F32)


def _sigmoid(x):
    return 1.0 / (1.0 + jnp.exp(-x))


def _gelu_tanh(x):
    return 0.5 * x * (1.0 + jnp.tanh(np.sqrt(2.0 / np.pi).astype(np.float32) * (x + 0.044715 * (x * x * x))))


def _rms_rows(x):
    return x * lax.rsqrt(jnp.mean(x * x, axis=-1, keepdims=True) + EPS)


def _const_spec(w):
    nd = w.ndim
    return pl.BlockSpec(w.shape, lambda *_: (0,) * nd)


def _layer_spec(w, layer):
    nd = w.ndim - 1
    return pl.BlockSpec((None,) + w.shape[1:], lambda *_: (layer,) + (0,) * nd)


def _compiler_params(semantics):
    return pltpu.CompilerParams(dimension_semantics=semantics, vmem_limit_bytes=VMEM_LIMIT_BYTES)


def _proj_kernel(x_ref, ca_ref, sb_ref, an_ref, wmix_ref, qan_ref, wuq_ref, kvan_ref, wuk_ref, wuvt_ref,
                 gqa_ref, gqb_ref, gka_ref, gkb_ref, gkn_ref, lng_ref, lnb_ref, indq_ref, indk_ref,
                 xr_ref, q_ref, k_ref, vt_ref, lat_ref, kpe_ref, ug_ref, vln_ref):
    width = xr_ref.shape[-1]
    lora = lat_ref.shape[-1]
    x = x_ref[...]
    h = (_rms_rows(x) * an_ref[...]).astype(BF16)
    z = _dot(h, wmix_ref[...])
    o = 0
    xr_ref[...] = z[:, o:o + width]
    o += width
    cq = z[:, o:o + lora]
    o += lora
    ckv = z[:, o:o + lora]
    o += lora
    zu = z[:, o:o + width]
    o += width
    zv = z[:, o:o + width]
    o += width
    zka = z[:, o:o + HEAD_BLOCK]
    o += HEAD_BLOCK
    zkb = z[:, o:o + HEAD_BLOCK]

    ca = ca_ref[...]
    sb = sb_ref[...]

    cqn = (_rms_rows(cq) * qan_ref[...]).astype(BF16)
    qab = _dot(cqn, wuq_ref[...])
    ta = ca * gqa_ref[...]
    tb = sb * gqb_ref[...]
    pair = 2 * HEAD_BLOCK
    for p in range(HEADS // 2):
        a = qab[:, p * pair:(p + 1) * pair]
        b = qab[:, HEADS * HEAD_BLOCK + p * pair:HEADS * HEAD_BLOCK + (p + 1) * pair]
        nrm = lax.rsqrt(_dot((a * a).astype(BF16), indq_ref[...]) + EPS)
        for e in range(2):
            sl = slice(e * HEAD_BLOCK, (e + 1) * HEAD_BLOCK)
            q_ref[:, p * pair + e * HEAD_BLOCK:p * pair + (e + 1) * HEAD_BLOCK] = (
                (a[:, sl] * ta + b[:, sl] * tb) * nrm[:, sl]).astype(q_ref.dtype)

    lat = _rms_rows(ckv) * kvan_ref[...]
    lat_ref[...] = lat
    latb = lat.astype(BF16)
    kx = _dot(latb, wuk_ref[...])
    vt = _dot_nt(wuvt_ref[...], latb)
    cw = vt_ref.shape[-1]
    for c in range(vt_ref.shape[0]):
        vt_ref[c] = vt[:, c * cw:(c + 1) * cw].astype(vt_ref.dtype)

    nk = lax.rsqrt(jnp.sum(zka * zka, axis=-1, keepdims=True) * (1.0 / QK_ROPE) + EPS)
    kpe = (zka * (ca * gka_ref[...]) + zkb * (sb * gkb_ref[...])) * nk
    kpe_ref[...] = kpe

    gkn = gkn_ref[...]
    for p in range(HEADS // 2):
        a = kx[:, p * pair:(p + 1) * pair]
        nrm = lax.rsqrt(_dot((a * a).astype(BF16), indk_ref[...]) + EPS)
        for e in range(2):
            sl = slice(e * HEAD_BLOCK, (e + 1) * HEAD_BLOCK)
            k_ref[:, p * pair + e * HEAD_BLOCK:p * pair + (e + 1) * HEAD_BLOCK] = (
                a[:, sl] * nrm[:, sl] * gkn + kpe).astype(k_ref.dtype)

    ug_ref[...] = _gelu_tanh(zu).astype(ug_ref.dtype)
    gv = _gelu_tanh(zv)
    mu = jnp.mean(gv, axis=-1, keepdims=True)
    gc = gv - mu
    var = jnp.mean(gc * gc, axis=-1, keepdims=True)
    vln_ref[...] = gc * lax.rsqrt(var + EPS) * lng_ref[...] + lnb_ref[...]


def _proj_call(x, ca, sb, pw, cw, layer, tm, q_dtype):
    n, d = x.shape
    width = pw['lng'].shape[-1]
    lora = pw['qan'].shape[-1]
    hw = HEADS * HEAD_BLOCK
    vw = HEADS * V_HEAD
    stacked = [pw[k] for k in ('an', 'wmix', 'qan', 'wuq', 'kvan', 'wuk', 'wuvt', 'gqa', 'gqb', 'gka', 'gkb',
                               'gkn', 'lng', 'lnb')]
    consts = [cw['indq'], cw['indk']]
    n_tab = ca.shape[0] // tm
    vchunk = LANES if tm % LANES == 0 else tm

    def row(i):
        return (i, 0)

    def tab(i):
        return (i % n_tab, 0)

    in_specs = ([pl.BlockSpec((tm, d), row), pl.BlockSpec((tm, HEAD_BLOCK), tab), pl.BlockSpec((tm, HEAD_BLOCK), tab)]
                + [_layer_spec(w, layer) for w in stacked] + [_const_spec(w) for w in consts])
    row_outs = [(width, F32), (hw, q_dtype), (hw, BF16), None, (lora, F32), (HEAD_BLOCK, F32), (width, BF16),
                (width, F32)]
    out_specs, out_shape = [], []
    for ro in row_outs:
        if ro is None:
            out_specs.append(pl.BlockSpec((tm // vchunk, vw, vchunk), lambda i: (i, 0, 0)))
            out_shape.append(jax.ShapeDtypeStruct((n // vchunk, vw, vchunk), BF16))
        else:
            out_specs.append(pl.BlockSpec((tm, ro[0]), row))
            out_shape.append(jax.ShapeDtypeStruct((n, ro[0]), ro[1]))
    return pl.pallas_call(
        _proj_kernel,
        grid=(n // tm,),
        in_specs=in_specs,
        out_specs=out_specs,
        out_shape=out_shape,
        compiler_params=_compiler_params(("parallel",)),
        name="proj",
    )(x, ca, sb, *stacked, *consts)


def _rglru_kernel(xr_ref, tail0_ref, h0_ref, cw_ref, cb_ref, wg_ref, bg_ref, sp_ref,
                  y_ref, hlast_ref, tail_s, h_s, a_s, b_s, hs_s, *, first_pos_is_zero):
    c = pl.program_id(0)
    bsz, tc, width = xr_ref.shape

    @pl.when(c == 0)
    def _():
        tail_s[...] = tail0_ref[...]
        h_s[...] = jnp.broadcast_to(h0_ref[...], h_s.shape)

    x = xr_ref[...]
    xp = jnp.concatenate([tail_s[...], x], axis=1)
    cw = cw_ref[...]
    xc = cb_ref[...] + x * cw[CONV_WIDTH - 1:CONV_WIDTH]
    for k in range(CONV_WIDTH - 1):
        shift = CONV_WIDTH - 1 - k
        xc = xc + xp[:, SUBLANES - shift:SUBLANES - shift + tc] * cw[k:k + 1]
    tail_s[...] = x[:, tc - SUBLANES:, :]

    xc2 = xc.reshape(bsz * tc, width)
    gates = _dot(xc2.astype(BF16), wg_ref[...]) + bg_ref[...]
    r = _sigmoid(gates[:, :width])
    i = _sigmoid(gates[:, width:])
    log_a = (-LRU_C) * r * sp_ref[...]
    a = jnp.exp(log_a)
    mult = jnp.sqrt(-jnp.tanh(log_a) * (a * a + 1.0))
    b = mult * (i * xc2)
    a_s[...] = a.reshape(bsz, tc, width)
    b_s[...] = b.reshape(bsz, tc, width)

    if first_pos_is_zero:
        @pl.when(c == 0)
        def _():
            b_s[:, 0:1, :] = (i * xc2).reshape(bsz, tc, width)[:, 0:1, :]

    row = lax.broadcasted_iota(jnp.int32, (bsz, SUBLANES, width), 1)

    def group(g, h):
        start = pl.multiple_of(g * SUBLANES, SUBLANES)
        av = a_s[:, pl.ds(start, SUBLANES), :]
        bv = b_s[:, pl.ds(start, SUBLANES), :]
        for dlt in (1, 2, 4):
            keep = row >= dlt
            a_sh = jnp.where(keep, pltpu.roll(av, dlt, axis=1), 1.0)
            b_sh = jnp.where(keep, pltpu.roll(bv, dlt, axis=1), 0.0)
            bv = av * b_sh + bv
            av = av * a_sh
        hall = av * h + bv
        hs_s[:, pl.ds(start, SUBLANES), :] = hall
        return jnp.broadcast_to(hall[:, SUBLANES - 1:SUBLANES, :], hall.shape)

    h = lax.fori_loop(0, tc // SUBLANES, group, h_s[...])
    h_s[...] = h
    y_ref[...] = hs_s[...].astype(y_ref.dtype)
    hlast_ref[...] = h[:, 0:1, :]


def _rglru_call(xr, tail0, h0, pw, layer, tc, first_pos_is_zero):
    bsz, t, width = xr.shape
    stacked = [pw[k] for k in ('cw', 'cb', 'wgate', 'bgate', 'sp')]
    kern = functools.partial(_rglru_kernel, first_pos_is_zero=first_pos_is_zero)
    return pl.pallas_call(
        kern,
        grid=(t // tc,),
        in_specs=[pl.BlockSpec((bsz, tc, width), lambda c: (0, c, 0)),
                  _const_spec(tail0), _const_spec(h0)] + [_layer_spec(w, layer) for w in stacked],
        out_specs=[pl.BlockSpec((bsz, tc, width), lambda c: (0, c, 0)),
                   pl.BlockSpec((bsz, 1, width), lambda c: (0, 0, 0))],
        out_shape=[jax.ShapeDtypeStruct((bsz, t, width), BF16), jax.ShapeDtypeStruct((bsz, 1, width), F32)],
        scratch_shapes=[pltpu.VMEM((bsz, SUBLANES, width), F32), pltpu.VMEM((bsz, SUBLANES, width), F32),
                        pltpu.VMEM((bsz, tc, width), F32), pltpu.VMEM((bsz, tc, width), F32),
                        pltpu.VMEM((bsz, tc, width), F32)],
        compiler_params=_compiler_params(("arbitrary",)),
        name="rglru",
    )(xr, tail0, h0, *stacked)


def _attn_prompt_kernel(q_ref, k_ref, vt_ref, o_ref, *scratch):
    m_refs, l_refs, acc_refs = scratch[:HEADS], scratch[HEADS:2 * HEADS], scratch[2 * HEADS:]
    qi = pl.program_id(1)
    tq = q_ref.shape[0]
    tk = vt_ref.shape[-1]
    per_q = tq // tk

    for hh in range(HEADS):
        m_refs[hh][...] = jnp.full((1, tq), NEG_INF, F32)
        l_refs[hh][...] = jnp.zeros((1, tq), F32)
        acc_refs[hh][...] = jnp.zeros((V_HEAD, tq), F32)

    def block(jb, masked):
        items = [(e, hh) for e in range(per_q) for hh in range(HEADS)]

        def scores(e, hh):
            start = pl.multiple_of((jb * per_q + e) * tk, tk)
            lanes = slice(hh * HEAD_BLOCK, (hh + 1) * HEAD_BLOCK)
            return _dot_nt(k_ref[pl.ds(start, tk), lanes], q_ref[:, lanes])

        pending = [scores(*it) for it in items[:ATTN_LOOKAHEAD]]
        for idx, (e, hh) in enumerate(items):
            if idx + ATTN_LOOKAHEAD < len(items):
                pending.append(scores(*items[idx + ATTN_LOOKAHEAD]))
            s = pending[idx]
            if masked:
                key = e * tk + lax.broadcasted_iota(jnp.int32, (tk, tq), 0)
                qry = lax.broadcasted_iota(jnp.int32, (tk, tq), 1)
                s = jnp.where(key <= qry, s, NEG_INF)
            m_old = m_refs[hh][...]
            m_new = jnp.maximum(m_old, jnp.max(s, axis=0, keepdims=True))
            p = jnp.exp2(s - m_new)
            alpha = jnp.exp2(m_old - m_new)
            l_refs[hh][...] = alpha * l_refs[hh][...] + jnp.sum(p, axis=0, keepdims=True)
            m_refs[hh][...] = m_new
            vt = vt_ref[jb * per_q + e, hh * V_HEAD:(hh + 1) * V_HEAD, :]
            acc_refs[hh][...] = alpha * acc_refs[hh][...] + _dot(vt, p.astype(BF16))

    def body(jb, carry):
        block(jb, False)
        return carry

    lax.fori_loop(0, qi, body, 0)
    block(qi, True)

    out_t = jnp.concatenate([acc_refs[hh][...] * (1.0 / l_refs[hh][...]) for hh in range(HEADS)], axis=0)
    o_ref[...] = jnp.transpose(out_t).astype(o_ref.dtype)


def _attn_prompt_call(q, k, vt, bsz, t, tq):
    n, hw = q.shape
    nq = t // tq
    vw, tk = vt.shape[1], vt.shape[2]
    return pl.pallas_call(
        _attn_prompt_kernel,
        grid=(bsz, nq),
        in_specs=[pl.BlockSpec((tq, hw), lambda b, i: (b * nq + i, 0)),
                  pl.BlockSpec((t, hw), lambda b, i: (b, 0)),
                  pl.BlockSpec((t // tk, vw, tk), lambda b, i: (b, 0, 0))],
        out_specs=pl.BlockSpec((tq, vw), lambda b, i: (b * nq + i, 0)),
        out_shape=jax.ShapeDtypeStruct((n, vw), BF16),
        scratch_shapes=([pltpu.VMEM((1, tq), F32)] * (2 * HEADS) + [pltpu.VMEM((V_HEAD, tq), F32)] * HEADS),
        compiler_params=_compiler_params(("parallel", "arbitrary")),
        name="attn_prompt",
    )(q, k, vt)


def _attn_sample_kernel(pt_ref, q_ref, qp_ref, latn_ref, kpen_ref, wukh_ref, gkn_ref, wukdh_ref, wuv_ref, ind_ref,
                        lat_hbm, kpet_hbm, o_ref, lat_buf, kpet_buf, sem, qa_s, m_s, l_s, acc_s,
                        *, layer, pages_per_chunk, chunks_per_step, n_steps):
    b = pl.program_id(0)
    sidx = pl.program_id(1)
    step = b * n_steps + sidx
    total_steps = pl.num_programs(0) * n_steps
    n_slots = lat_buf.shape[0]
    ahead = n_slots - 1
    hq = qa_s.shape[0]
    t_new = hq // HEADS
    n_fold = wukdh_ref.shape[-1] // LANES

    def slot_of(gstep, chunk):
        if chunks_per_step % n_slots == 0:
            return chunk % n_slots
        return (gstep * chunks_per_step + chunk) % n_slots

    def copies(gstep, chunk):
        bb = gstep // n_steps
        page0 = ((gstep % n_steps) * chunks_per_step + chunk) * pages_per_chunk
        sl = slot_of(gstep, chunk)
        out = []
        for i in range(pages_per_chunk):
            page = pt_ref[bb, page0 + i]
            out.append(pltpu.make_async_copy(lat_hbm.at[layer, page],
                                             lat_buf.at[sl, pl.ds(i * PAGE_SIZE, PAGE_SIZE)], sem.at[0, sl]))
            out.append(pltpu.make_async_copy(kpet_hbm.at[layer, page],
                                             kpet_buf.at[sl, :, pl.ds(i * PAGE_SIZE, PAGE_SIZE)], sem.at[1, sl]))
        return out

    @pl.when(step == 0)
    def _():
        for chunk in range(ahead):
            for cp in copies(step, chunk):
                cp.start()

    @pl.when(sidx == 0)
    def _():
        m_s[...] = jnp.full(m_s.shape, NEG_INF, F32)
        l_s[...] = jnp.zeros(l_s.shape, F32)
        acc_s[...] = jnp.zeros(acc_s.shape, F32)
        q = q_ref[0]
        for hh in range(HEADS):
            qn = (q[:, hh * HEAD_BLOCK:hh * HEAD_BLOCK + QK_NOPE] * gkn_ref[...]).astype(BF16)
            qa_s[hh * t_new:(hh + 1) * t_new, :] = _dot_nt(qn, wukh_ref[hh])

    qp = qp_ref[0]
    qa = qa_s[...].astype(BF16)

    def expand(lat, rope_scores):
        latb = lat.astype(BF16)
        kx = _dot(latb, wukdh_ref[...])
        return latb, kx, _dot_nt(qa, latb), rope_scores()

    def fold_squares(kx):
        fold = kx[:, :LANES] * kx[:, :LANES]
        for f in range(1, n_fold):
            part = kx[:, f * LANES:(f + 1) * LANES]
            fold = fold + part * part
        return fold.astype(BF16)

    def mean_square(fold):
        return _dot_nt(ind_ref[...], fold)

    def softmax_update(latb, s_abs, s_rope, ms, mask):
        s = s_abs * lax.rsqrt(ms + EPS) + s_rope
        if mask is not None:
            s = jnp.where(mask, s, NEG_INF)
        m_old = m_s[...]
        m_new = jnp.maximum(m_old, jnp.max(s, axis=1, keepdims=True))
        p = jnp.exp2(s - m_new)
        alpha = jnp.exp2(m_old - m_new)
        l_s[...] = alpha * l_s[...] + jnp.sum(p, axis=1, keepdims=True)
        m_s[...] = m_new
        acc_s[...] = acc_s[...] * alpha + _dot(p.astype(BF16), latb)

    prev = None
    for chunk in range(chunks_per_step):
        tgt = chunk + ahead
        if tgt < chunks_per_step:
            for cp in copies(step, tgt):
                cp.start()
        else:
            @pl.when(step + 1 < total_steps)
            def _(tgt=tgt):
                for cp in copies(step + 1, tgt - chunks_per_step):
                    cp.start()
        for cp in copies(step, chunk):
            cp.wait()
        sl = slot_of(step, chunk)
        ms_prev = mean_square(prev[3]) if prev is not None else None
        latb, kx, s_abs, s_rope = expand(lat_buf[sl], lambda sl=sl: _dot(qp, kpet_buf[sl].astype(BF16)))
        if prev is not None:
            softmax_update(prev[0], prev[1], prev[2], ms_prev, None)
        prev = (latb, s_abs, s_rope, fold_squares(kx))
    softmax_update(prev[0], prev[1], prev[2], mean_square(prev[3]), None)

    @pl.when(sidx == n_steps - 1)
    def _():
        qry = lax.broadcasted_iota(jnp.int32, (hq, t_new), 0) % t_new
        key = lax.broadcasted_iota(jnp.int32, (hq, t_new), 1)
        latb, kx, s_abs, s_rope = expand(latn_ref[0], lambda: _dot_nt(qp, kpen_ref[0].astype(BF16)))
        softmax_update(latb, s_abs, s_rope, mean_square(fold_squares(kx)), key <= qry)
        ol = acc_s[...] / l_s[...]
        for hh in range(HEADS):
            o_ref[0, :, hh * V_HEAD:(hh + 1) * V_HEAD] = _dot(
                ol[hh * t_new:(hh + 1) * t_new].astype(BF16), wuv_ref[hh]).astype(o_ref.dtype)


def _attn_sample_call(page_table, q, qp, lat_new, kpe_new, pw, cw, cache_lat, cache_kpe_t, layer):
    bsz, n_pages = page_table.shape
    pages_per_chunk = _tile_rows(n_pages, SAMPLE_CHUNK_PAGES)
    chunks = n_pages // pages_per_chunk
    chunks_per_step = _tile_rows(chunks, SAMPLE_CHUNKS_PER_STEP)
    n_slots = min(SAMPLE_RING_SLOTS, chunks_per_step + 1)
    n_steps = chunks // chunks_per_step
    ck = pages_per_chunk * PAGE_SIZE
    lora = lat_new.shape[-1]
    t_new = lat_new.shape[1]
    hq = HEADS * t_new
    stacked = [pw[k] for k in ('wukh', 'gkn64', 'wukdh', 'wuvh')]
    consts = [cw['ind_s']]
    kern = functools.partial(_attn_sample_kernel, layer=layer, pages_per_chunk=pages_per_chunk,
                             chunks_per_step=chunks_per_step, n_steps=n_steps)

    def per_batch(a):
        nd = a.ndim - 1
        return pl.BlockSpec((1,) + a.shape[1:], lambda b, c, pt: (b,) + (0,) * nd)

    def layer_spec(w):
        nd = w.ndim - 1
        return pl.BlockSpec((None,) + w.shape[1:], lambda b, c, pt: (layer,) + (0,) * nd)

    def const_spec(w):
        nd = w.ndim
        return pl.BlockSpec(w.shape, lambda b, c, pt: (0,) * nd)

    grid_spec = pltpu.PrefetchScalarGridSpec(
        num_scalar_prefetch=1,
        grid=(bsz, n_steps),
        in_specs=[per_batch(q), per_batch(qp), per_batch(lat_new), per_batch(kpe_new)]
        + [layer_spec(w) for w in stacked] + [const_spec(w) for w in consts]
        + [pl.BlockSpec(memory_space=pl.ANY), pl.BlockSpec(memory_space=pl.ANY)],
        out_specs=pl.BlockSpec((1, t_new, HEADS * V_HEAD), lambda b, c, pt: (b, 0, 0)),
        scratch_shapes=[pltpu.VMEM((n_slots, ck, lora), F32), pltpu.VMEM((n_slots, QK_ROPE, ck), F32),
                        pltpu.SemaphoreType.DMA((2, n_slots)),
                        pltpu.VMEM((hq, lora), F32),
                        pltpu.VMEM((hq, 1), F32), pltpu.VMEM((hq, 1), F32), pltpu.VMEM((hq, lora), F32)],
    )
    return pl.pallas_call(
        kern,
        grid_spec=grid_spec,
        out_shape=jax.ShapeDtypeStruct((bsz, t_new, HEADS * V_HEAD), BF16),
        compiler_params=_compiler_params(("arbitrary", "arbitrary")),
        name="attn_sample",
    )(page_table, q, qp, lat_new, kpe_new, *stacked, *consts, cache_lat, cache_kpe_t)


def _merge_kernel(x_ref, yrg_ref, yat_ref, ug_ref, vln_ref, an_ref, wg_ref, wb_ref, wout_ref, sgw_ref, sgb_ref,
                  o_ref, ycm_s):
    tm, d = x_ref.shape
    period = sgw_ref.shape[-1]
    gdim = vln_ref.shape[-1] // SGU_GROUPS
    x = x_ref[...]
    h = (_rms_rows(x) * an_ref[...]).astype(BF16)
    gates = _dot(h, wg_ref[...])

    for cidx in range(tm // period):
        rows = slice(cidx * period, (cidx + 1) * period)
        for g in range(SGU_GROUPS):
            cols = slice(g * gdim, (g + 1) * gdim)
            s = _dot(sgw_ref[g], vln_ref[rows, cols].astype(BF16)) + sgb_ref[:, cols]
            ycm_s[rows, cols] = (ug_ref[rows, cols].astype(F32) * s).astype(BF16)

    merged = _sigmoid(gates[:, :d]) * _dot(yrg_ref[...], wb_ref[0])
    merged = merged + _sigmoid(gates[:, d:2 * d]) * _dot(yat_ref[...], wb_ref[1])
    merged = merged + _sigmoid(gates[:, 2 * d:]) * _dot(ycm_s[...], wb_ref[2])
    o_ref[...] = x + _dot(merged.astype(BF16), wout_ref[...])


def _merge_call(x, yrg, yat, ug, vln, pw, sgw, sgb, layer, tm):
    n, d = x.shape
    width = yrg.shape[-1]
    stacked = [pw['an'], pw['wgates'], pw['wbranch'], pw['wout'], sgw, sgb]

    def row(i):
        return (i, 0)

    return pl.pallas_call(
        _merge_kernel,
        grid=(n // tm,),
        in_specs=[pl.BlockSpec((tm, d), row)] + [pl.BlockSpec((tm, width), row)] * 4
        + [_layer_spec(w, layer) for w in stacked],
        out_specs=pl.BlockSpec((tm, d), row),
        out_shape=jax.ShapeDtypeStruct((n, d), F32),
        scratch_shapes=[pltpu.VMEM((tm, width), BF16)],
        compiler_params=_compiler_params(("parallel",)),
        name="merge",
    )(x, yrg, yat, ug, vln, *stacked)


def _ffn_kernel(x_ref, fn_ref, wgu_ref, wd_ref, o_ref):
    dff = wd_ref.shape[0]
    x = x_ref[...]
    h = (_rms_rows(x) * fn_ref[...]).astype(BF16)
    gu = _dot(h, wgu_ref[...])
    gate = gu[:, :dff]
    act = (gate * _sigmoid(gate) * gu[:, dff:]).astype(BF16)
    o_ref[...] = x + _dot(act, wd_ref[...])


def _ffn_call(x, pw, layer, tm):
    n, d = x.shape
    stacked = [pw['fn'], pw['wgu'], pw['wdown']]

    def row(i):
        return (i, 0)

    return pl.pallas_call(
        _ffn_kernel,
        grid=(n // tm,),
        in_specs=[pl.BlockSpec((tm, d), row)] + [_layer_spec(w, layer) for w in stacked],
        out_specs=pl.BlockSpec((tm, d), row),
        out_shape=jax.ShapeDtypeStruct((n, d), F32),
        compiler_params=_compiler_params(("parallel",)),
        name="ffn",
    )(x, *stacked)


def _head_blocks(parts):
    used = sum(p.shape[-1] for p in parts)
    pad = jnp.zeros(parts[0].shape[:-1] + (HEAD_BLOCK - used,), parts[0].dtype)
    blk = jnp.concatenate(list(parts) + [pad], axis=-1)
    return blk.reshape(blk.shape[:-2] + (HEADS * HEAD_BLOCK,))


def _block_vec(parts):
    used = sum(p.shape[-1] for p in parts)
    pad = jnp.zeros((parts[0].shape[0], HEAD_BLOCK - used), F32)
    return jnp.concatenate(list(parts) + [pad], axis=-1)[:, None, :]


def _pack_weights(p, width, lora, t_new):
    w_in = p['w_in']
    depth, d, _ = w_in.shape
    o_cq = width
    o_ckv = o_cq + lora
    o_kpe = o_ckv + lora
    o_u = o_kpe + QK_ROPE
    o_v = o_u + width
    o_g = o_v + width
    wk = w_in[..., o_kpe:o_u]
    zn = jnp.zeros((depth, d, QK_NOPE), F32)
    zr = jnp.zeros((depth, d, HEAD_BLOCK - QK_HEAD), F32)
    k1, k2 = wk[..., :ROPE_HALF], wk[..., ROPE_HALF:]
    wmix = jnp.concatenate([w_in[..., :o_kpe], w_in[..., o_u:o_g], zn, k1, k2, zr, zn, k2, k1, zr], axis=-1)

    wuq = p['w_uq'].reshape(depth, lora, HEADS, QK_HEAD)
    qn, q1, q2 = wuq[..., :QK_NOPE], wuq[..., QK_NOPE:QK_NOPE + ROPE_HALF], wuq[..., QK_NOPE + ROPE_HALF:]
    wuq_ab = jnp.concatenate([_head_blocks([qn, q1, q2]), _head_blocks([jnp.zeros_like(qn), q2, q1])], axis=-1)

    wukv = p['w_ukv'].reshape(depth, lora, HEADS, QK_NOPE + V_HEAD)
    kn, vv = wukv[..., :QK_NOPE], wukv[..., QK_NOPE:]

    lane = np.arange(2 * HEAD_BLOCK) % HEAD_BLOCK
    blk = np.arange(2 * HEAD_BLOCK) // HEAD_BLOCK
    same = blk[:, None] == blk[None, :]
    nope = lane < QK_NOPE
    ropel = (lane >= QK_NOPE) & (lane < QK_HEAD)
    ind_nope = (same & nope[:, None] & nope[None, :]) / QK_NOPE
    ind_rope = (same & ropel[:, None] & ropel[None, :]) / QK_ROPE

    q_scale = ATTN_SCALE * LOG2_E
    gq_n, gq_r = p['q_nope_norm'] * q_scale, p['q_rope_norm'] * q_scale
    gk_r = p['k_rope_norm']
    zero_n = jnp.zeros((depth, QK_NOPE), F32)

    bw = width // LRU_BLOCKS
    eye = jnp.eye(LRU_BLOCKS, dtype=F32)

    def block_diag(w):
        return jnp.einsum('lncd,nm->lncmd', w, eye).reshape(depth, width, width)

    lam = p['lru_lambda']
    sp = jnp.maximum(-lam, 0.0) + jnp.log1p(jnp.exp(-jnp.abs(lam)))

    hq = HEADS * t_new
    fold_head = np.arange(LANES) % HEADS
    ind_s = ((np.arange(hq) // t_new)[:, None] == fold_head[None, :]) / QK_NOPE

    def vec(a):
        return a.reshape(depth, 1, -1)

    stacked = dict(
        an=vec(p['attn_norm']),
        wmix=wmix.astype(BF16),
        wgates=w_in[..., o_g:].astype(BF16),
        qan=vec(p['q_a_norm']),
        wuq=wuq_ab.astype(BF16),
        kvan=vec(p['kv_a_norm']),
        wuk=_head_blocks([kn]).astype(BF16),
        wuvt=jnp.swapaxes(vv.reshape(depth, lora, HEADS * V_HEAD), 1, 2).astype(BF16),
        gqa=_block_vec([gq_n, gq_r]),
        gqb=_block_vec([zero_n, gq_r[:, ROPE_HALF:], gq_r[:, :ROPE_HALF]]),
        gka=_block_vec([zero_n, gk_r]),
        gkb=_block_vec([zero_n, gk_r[:, ROPE_HALF:], gk_r[:, :ROPE_HALF]]),
        gkn=_block_vec([p['k_nope_norm']]),
        lng=vec(p['sgu_ln_g']),
        lnb=vec(p['sgu_ln_b']),
        cw=p['conv_w'],
        cb=vec(p['conv_b']),
        wgate=jnp.concatenate([block_diag(p['lru_gate_a_w']), block_diag(p['lru_gate_x_w'])], axis=-1).astype(BF16),
        bgate=jnp.concatenate([p['lru_gate_a_b'].reshape(depth, 1, width),
                               p['lru_gate_x_b'].reshape(depth, 1, width)], axis=-1),
        sp=vec(sp),
        wbranch=p['w_branch'].astype(BF16),
        wout=p['w_out'].astype(BF16),
        fn=vec(p['ffn_norm']),
        wgu=jnp.concatenate([p['w_gate'], p['w_up']], axis=-1).astype(BF16),
        wdown=p['w_down'].astype(BF16),
        wukh=jnp.transpose(kn, (0, 2, 1, 3)).astype(BF16),
        gkn64=vec(p['k_nope_norm']),
        wukdh=jnp.swapaxes(kn, 2, 3).reshape(depth, lora, HEADS * QK_NOPE).astype(BF16),
        wuvh=jnp.transpose(vv, (0, 2, 1, 3)).astype(BF16),
    )
    consts = dict(
        indq=jnp.asarray(ind_nope + ind_rope, BF16),
        indk=jnp.asarray(ind_nope, BF16),
        ind_s=jnp.asarray(ind_s, BF16),
    )
    return stacked, consts


def _sgu_matrices(sgu_w, sgu_b, period, chunk_len, gdim):
    depth, groups = sgu_w.shape[:2]
    reps = period // chunk_len
    tri = jnp.tril(jnp.ones((chunk_len, chunk_len), F32))
    w = sgu_w[:, :, :chunk_len, :chunk_len] * tri
    mat = jnp.einsum('lgts,rq->lgrtqs', w, jnp.eye(reps, dtype=F32)).reshape(depth, groups, period, period)
    bias = jnp.tile(jnp.swapaxes(sgu_b[:, :, :chunk_len], 1, 2), (1, reps, 1))
    return mat.astype(BF16), jnp.repeat(bias, gdim, axis=2)


def _rope_tables(pos):
    inv = ROPE_THETA ** (-jnp.arange(ROPE_HALF, dtype=F32) / ROPE_HALF)
    ang = pos.astype(F32)[:, None] * inv[None, :]
    cos, sin = jnp.cos(ang), jnp.sin(ang)
    n = pos.shape[0]
    ones = jnp.ones((n, QK_NOPE), F32)
    zn = jnp.zeros((n, QK_NOPE), F32)
    zr = jnp.zeros((n, HEAD_BLOCK - QK_HEAD), F32)
    return (jnp.concatenate([ones, cos, cos, zr], axis=-1), jnp.concatenate([zn, -sin, sin, zr], axis=-1))


def _tile_rows(n, target):
    t = min(n, target)
    while n % t:
        t //= 2
    return t


def kernel(x_prompt, x_sample, cache_kv_latent, cache_k_rope, page_table, state_rglru_h, state_rglru_conv,
           attn_norm, w_in, conv_w, conv_b, lru_gate_a_w, lru_gate_a_b, lru_gate_x_w, lru_gate_x_b, lru_lambda,
           q_a_norm, w_uq, kv_a_norm, w_ukv, q_nope_norm, q_rope_norm, k_nope_norm, k_rope_norm,
           sgu_ln_g, sgu_ln_b, sgu_w, sgu_b, w_branch, w_out, ffn_norm, w_gate, w_up, w_down):
    params = dict(attn_norm=attn_norm, w_in=w_in, conv_w=conv_w, conv_b=conv_b, lru_gate_a_w=lru_gate_a_w,
                  lru_gate_a_b=lru_gate_a_b, lru_gate_x_w=lru_gate_x_w, lru_gate_x_b=lru_gate_x_b,
                  lru_lambda=lru_lambda, q_a_norm=q_a_norm, w_uq=w_uq, kv_a_norm=kv_a_norm, w_ukv=w_ukv,
                  q_nope_norm=q_nope_norm, q_rope_norm=q_rope_norm, k_nope_norm=k_nope_norm,
                  k_rope_norm=k_rope_norm, sgu_ln_g=sgu_ln_g, sgu_ln_b=sgu_ln_b, w_branch=w_branch, w_out=w_out,
                  ffn_norm=ffn_norm, w_gate=w_gate, w_up=w_up, w_down=w_down)
    depth = w_in.shape[0]
    bp, tp, d = x_prompt.shape
    bs, ts, _ = x_sample.shape
    width = conv_w.shape[-1]
    lora = q_a_norm.shape[-1]
    n_pages = page_table.shape[1]
    past = n_pages * PAGE_SIZE
    assert tp % CHUNK == 0 and ts <= CHUNK and past % CHUNK == 0 and ts == SUBLANES
    n_p, n_s = bp * tp, bs * ts
    gdim = width // SGU_GROUPS

    tm_p = _tile_rows(tp, 256)
    tm_s = n_s
    tq = _tile_rows(tp, 256)
    tc_p = _tile_rows(tp, 128)

    pw, cw = _pack_weights(params, width, lora, ts)
    sgw_p, sgb_p = _sgu_matrices(sgu_w, sgu_b, CHUNK, CHUNK, gdim)
    sgw_s, sgb_s = _sgu_matrices(sgu_w, sgu_b, tm_s, ts, gdim)

    ca_p, sb_p = _rope_tables(jnp.arange(tp))
    ca_s, sb_s = _rope_tables(past + jnp.arange(ts))
    ca_s, sb_s = jnp.tile(ca_s, (bs, 1)), jnp.tile(sb_s, (bs, 1))

    cache_kpe_t = jnp.swapaxes(cache_k_rope, 2, 3)
    tail_s = jnp.concatenate([jnp.zeros((depth, bs, SUBLANES - (CONV_WIDTH - 1), width), F32), state_rglru_conv],
                             axis=2)
    tail_p = jnp.zeros((bp, SUBLANES, width), F32)
    h0_p = jnp.zeros((bp, 1, width), F32)

    xp = x_prompt.reshape(n_p, d)
    xs = x_sample.reshape(n_s, d)
    outs = {k: [] for k in ('lat_p', 'kpe_p', 'h_p', 'conv_p', 'lat_s', 'kpe_s', 'h_s', 'conv_s', 'v_s')}
    rope_lanes = slice(QK_NOPE, QK_HEAD)

    for l in range(depth):
        xr, q, k, vt, lat, kpe, ug, vln = _proj_call(xp, ca_p, sb_p, pw, cw, l, tm_p, BF16)
        xr3 = xr.reshape(bp, tp, width)
        yrg, hlast = _rglru_call(xr3, tail_p, h0_p, pw, l, tc_p, True)
        yat = _attn_prompt_call(q, k, vt, bp, tp, tq)
        x1 = _merge_call(xp, yrg.reshape(n_p, width), yat, ug, vln, pw, sgw_p, sgb_p, l, tm_p)
        xp = _ffn_call(x1, pw, l, tm_p)
        outs['lat_p'].append(lat.reshape(bp, tp, lora))
        outs['kpe_p'].append(kpe[:, rope_lanes].reshape(bp, tp, QK_ROPE))
        outs['h_p'].append(hlast.reshape(bp, width))
        outs['conv_p'].append(xr3[:, tp - (CONV_WIDTH - 1):, :])

        xr, q, _, _, lat, kpe, ug, vln = _proj_call(xs, ca_s, sb_s, pw, cw, l, tm_s, F32)
        xr3 = xr.reshape(bs, ts, width)
        yrg, hlast = _rglru_call(xr3, tail_s[l], state_rglru_h[l][:, None, :], pw, l, ts, past == 0)
        q3 = q.reshape(bs, ts, HEADS * HEAD_BLOCK)
        qp = jnp.transpose(q3.reshape(bs, ts, HEADS, HEAD_BLOCK)[..., rope_lanes], (0, 2, 1, 3))
        qp = qp.reshape(bs, HEADS * ts, QK_ROPE).astype(BF16)
        lat3 = lat.reshape(bs, ts, lora)
        kpe3 = kpe[:, rope_lanes].reshape(bs, ts, QK_ROPE)
        yat = _attn_sample_call(page_table, q3, qp, lat3, kpe3, pw, cw, cache_kv_latent, cache_kpe_t, l)
        x1 = _merge_call(xs, yrg.reshape(n_s, width), yat.reshape(n_s, width), ug, vln, pw, sgw_s, sgb_s, l, tm_s)
        xs = _ffn_call(x1, pw, l, tm_s)
        outs['lat_s'].append(lat3)
        outs['kpe_s'].append(kpe3)
        outs['h_s'].append(hlast.reshape(bs, width))
        outs['conv_s'].append(xr3[:, ts - (CONV_WIDTH - 1):, :])
        outs['v_s'].append(vln.reshape(bs, ts, width))

    st = {k: jnp.stack(v) for k, v in outs.items()}
    return (xp.reshape(bp, tp, d), xs.reshape(bs, ts, d),
            st['lat_p'], st['kpe_p'], st['h_p'], st['conv_p'],
            st['lat_s'], st['kpe_s'], st['h_s'], st['conv_s'], st['v_s'])
```
